```python
import math
import jax, jax.numpy as jnp
from jax import lax
import numpy as np

D_MODEL = 1024
BATCH = 16
SEQ = 2048
DEPTH = 2

HEAD_DIM = 64
Q_BLOCK = 128
ROPE_THETA = 10000.0
LN_EPS = 1e-5
RMS_EPS = 1e-6
MLA_HEADS = 4
MLA_Q_LORA = 256
MLA_KV_LORA = 128
MLA_NOPE = 64
MLA_ROPE = 32
MLA_V = 64
SWA_HEADS = 4
SWA_KV_HEADS = 2
SWA_WINDOW = 128
NSA_HEADS = 8
NSA_KV_HEADS = 2
NSA_BLOCK = 32
NSA_TOP_N = 8
NSA_WINDOW = 128
NSA_PHI_HIDDEN = 256
N_BRANCH = 3
FORCE_SCORE = 1e4
D_MIX = MLA_HEADS * MLA_V + SWA_HEADS * HEAD_DIM + NSA_HEADS * HEAD_DIM
IN_SIZES = (MLA_Q_LORA, MLA_KV_LORA, MLA_ROPE,
            SWA_HEADS * HEAD_DIM, SWA_KV_HEADS * HEAD_DIM, SWA_KV_HEADS * HEAD_DIM,
            NSA_HEADS * HEAD_DIM,
            NSA_KV_HEADS * HEAD_DIM, NSA_KV_HEADS * HEAD_DIM,
            NSA_KV_HEADS * HEAD_DIM, NSA_KV_HEADS * HEAD_DIM,
            NSA_KV_HEADS * HEAD_DIM, NSA_KV_HEADS * HEAD_DIM,
            NSA_HEADS * N_BRANCH)
D_IN = sum(IN_SIZES)
N_EXPERTS = 16
N_GROUPS = 4
EXPERTS_PER_GROUP = N_EXPERTS // N_GROUPS
TOP_K = 2
D_EXPERT = 256
ALPHA = (2 * DEPTH) ** 0.25
BETA = (8 * DEPTH) ** -0.25
MAX_POS_OFFSET = 4096

kernel_name = "hybrid_mla_swa_nsa_moe_deepnorm_adaln"


def _ln(x):
    xf = x.astype(jnp.float32)
    mu = jnp.mean(xf, -1, keepdims=True)
    var = jnp.mean(jnp.square(xf - mu), -1, keepdims=True)
    return (xf - mu) * lax.rsqrt(var + LN_EPS)


def layernorm(x, g, b):
    return (_ln(x) * g + b).astype(x.dtype)


def rmsnorm(x, g):
    xf = x.astype(jnp.float32)
    y = xf * lax.rsqrt(jnp.mean(jnp.square(xf), -1, keepdims=True) + RMS_EPS)
    return (y * g).astype(x.dtype)


def rope_tables(positions, dim):
    inv_freq = 1.0 / (ROPE_THETA ** (jnp.arange(0, dim, 2, dtype=jnp.float32) / dim))
    ang = positions.astype(jnp.float32)[..., None] * inv_freq
    return jnp.cos(ang)[:, :, None, :], jnp.sin(ang)[:, :, None, :]


def apply_rope(x, cos, sin):
    xf = x.astype(jnp.float32)
    x1, x2 = jnp.split(xf, 2, axis=-1)
    return jnp.concatenate([x1 * cos - x2 * sin, x2 * cos + x1 * sin], -1).astype(x.dtype)


def masked_softmax(s, mask):
    s = jnp.where(mask, s, -jnp.inf)
    m = jnp.max(s, axis=-1, keepdims=True)
    m = jnp.where(jnp.isfinite(m), m, 0.0)
    e = jnp.exp(s - m)
    den = jnp.sum(e, axis=-1, keepdims=True)
    return e / jnp.where(den > 0, den, 1.0)


def to_blocks(x):
    B, T = x.shape[:2]
    return jnp.moveaxis(x.reshape(B, T // Q_BLOCK, Q_BLOCK, *x.shape[2:]), 1, 0)


def from_blocks(y):
    NQ, B, Q = y.shape[:3]
    return jnp.moveaxis(y, 0, 1).reshape(B, NQ * Q, *y.shape[3:])


def mla_attention(c_q, c_kv, k_pe, cos_r, sin_r, q_norm, w_uq, kv_norm, w_ukv):
    B, T, _ = c_q.shape
    q = (rmsnorm(c_q, q_norm) @ w_uq).reshape(B, T, MLA_HEADS, MLA_NOPE + MLA_ROPE)
    q_nope = q[..., :MLA_NOPE]
    q_pe = apply_rope(q[..., MLA_NOPE:], cos_r, sin_r)
    kv = (rmsnorm(c_kv, kv_norm) @ w_ukv).reshape(B, T, MLA_HEADS, MLA_NOPE + MLA_V)
    k_nope, v = kv[..., :MLA_NOPE], kv[..., MLA_NOPE:]
    k_pe = apply_rope(k_pe[:, :, None, :], cos_r, sin_r)[:, :, 0]
    scale = 1.0 / math.sqrt(MLA_NOPE + MLA_ROPE)
    kpos = jnp.arange(T)

    def block(args):
        qn, qp, q0 = args
        s = (jnp.einsum('bqhd,bkhd->bhqk', qn, k_nope)
             + jnp.einsum('bqhd,bkd->bhqk', qp, k_pe)).astype(jnp.float32) * scale
        qpos = q0 + jnp.arange(Q_BLOCK)
        p = masked_softmax(s, kpos[None, :] <= qpos[:, None])
        return jnp.einsum('bhqk,bkhd->bqhd', p.astype(v.dtype), v)

    starts = jnp.arange(T // Q_BLOCK, dtype=jnp.int32) * Q_BLOCK
    o = lax.map(block, (to_blocks(q_nope), to_blocks(q_pe), starts))
    return from_blocks(o).reshape(B, T, MLA_HEADS * MLA_V)


def banded_window_attention(q, k, v, window, sinks=None):
    B, T, Hq, d = q.shape
    Hkv = k.shape[2]
    R = Hq // Hkv
    NB = T // Q_BLOCK
    n_prev = -(-window // Q_BLOCK)
    L = (n_prev + 1) * Q_BLOCK
    pad = ((0, 0), (n_prev * Q_BLOCK, 0), (0, 0), (0, 0))
    kp = jnp.pad(k, pad).reshape(B, NB + n_prev, Q_BLOCK, Hkv, d)
    vp = jnp.pad(v, pad).reshape(B, NB + n_prev, Q_BLOCK, Hkv, d)
    kb = jnp.concatenate([kp[:, i:i + NB] for i in range(n_prev + 1)], axis=2)
    vb = jnp.concatenate([vp[:, i:i + NB] for i in range(n_prev + 1)], axis=2)
    qb = q.reshape(B, NB, Q_BLOCK, Hkv, R, d)
    s = jnp.einsum('bnqgrd,bnkgd->bngrqk', qb, kb).astype(jnp.float32) / math.sqrt(d)
    qi = jnp.arange(Q_BLOCK)[:, None]
    ki = jnp.arange(L)[None, :]
    rel = qi + n_prev * Q_BLOCK - ki
    kpos = jnp.arange(NB)[:, None, None] * Q_BLOCK - n_prev * Q_BLOCK + ki[None]
    mask = (((rel >= 0) & (rel < window))[None] & (kpos >= 0))[None, :, None, None]
    if sinks is None:
        p = masked_softmax(s, mask)
    else:
        sink = sinks.astype(jnp.float32).reshape(1, 1, Hkv, R, 1, 1)
        s = jnp.where(mask, s, -jnp.inf)
        m = jnp.maximum(jnp.max(s, -1, keepdims=True), sink)
        e = jnp.exp(s - m)
        p = e / (jnp.sum(e, -1, keepdims=True) + jnp.exp(sink - m))
    o = jnp.einsum('bngrqk,bnkgd->bnqgrd', p.astype(v.dtype), vb)
    return o.reshape(B, T, Hq * d)


def nsa_attention(q, kc_raw, vc_raw, ks, vs, kw, vw, gate_logits,
                  cmp_pos, phi_k1, phi_k2, phi_v1, phi_v2):
    B, T, H, d = q.shape
    G = NSA_KV_HEADS
    R = H // G
    NC = T // NSA_BLOCK
    scale = 1.0 / math.sqrt(d)
    qg = q.reshape(B, T, G, R, d)

    def compress(z, w1, w2):
        zb = z.reshape(B, NC, NSA_BLOCK, G, d) + cmp_pos[None, None, :, None, :]
        zb = jnp.moveaxis(zb, 3, 2).reshape(B, NC, G, NSA_BLOCK * d)
        return jax.nn.gelu(zb @ w1) @ w2

    k_cmp = compress(kc_raw, phi_k1, phi_k2)
    v_cmp = compress(vc_raw, phi_v1, phi_v2)

    t = jnp.arange(T)
    j = jnp.arange(NC)
    cmp_mask = (j[None, :] * NSA_BLOCK + NSA_BLOCK - 1) <= t[:, None]
    s_cmp = jnp.einsum('btgrd,bcgd->bgrtc', qg, k_cmp).astype(jnp.float32) * scale
    p_cmp = masked_softmax(s_cmp, cmp_mask)
    o_cmp = jnp.einsum('bgrtc,bcgd->btgrd', p_cmp.astype(v_cmp.dtype), v_cmp)

    imp = jnp.sum(p_cmp, axis=2)
    tb = t // NSA_BLOCK
    future = j[None, :] * NSA_BLOCK > t[:, None]
    forced = (j[None, :] == 0) | (j[None, :] == tb[:, None]) | (j[None, :] == tb[:, None] - 1)
    imp = jnp.where(future, -jnp.inf, jnp.where(forced, FORCE_SCORE, imp))
    n_sel = min(NSA_TOP_N, NC)
    _, sel = lax.top_k(imp, n_sel)

    k_blk = jnp.moveaxis(ks.reshape(B, NC, NSA_BLOCK, G, d), 3, 1)
    v_blk = jnp.moveaxis(vs.reshape(B, NC, NSA_BLOCK, G, d), 3, 1)
    gather = jax.vmap(jax.vmap(lambda tbl, idx: tbl[idx]))
    M = n_sel * NSA_BLOCK
    sel_b = jnp.moveaxis(sel.reshape(B, G, T // Q_BLOCK, Q_BLOCK, n_sel), 2, 0)
    starts = jnp.arange(T // Q_BLOCK, dtype=jnp.int32) * Q_BLOCK

    def block(args):
        qc, idx, q0 = args
        kg = gather(k_blk, idx).reshape(B, G, Q_BLOCK, M, d)
        vg = gather(v_blk, idx).reshape(B, G, Q_BLOCK, M, d)
        s = jnp.einsum('bqgrd,bgqmd->bgrqm', qc, kg).astype(jnp.float32) * scale
        kpos = (idx[..., None] * NSA_BLOCK + jnp.arange(NSA_BLOCK)).reshape(B, G, Q_BLOCK, M)
        qpos = q0 + jnp.arange(Q_BLOCK)
        mask = (kpos <= qpos[None, None, :, None])[:, :, None]
        p = masked_softmax(s, mask)
        return jnp.einsum('bgrqm,bgqmd->bqgrd', p.astype(vg.dtype), vg)

    o_slc = from_blocks(lax.map(block, (to_blocks(qg), sel_b, starts)))
    o_win = banded_window_attention(q, kw, vw, NSA_WINDOW).reshape(B, T, H, d)
    g = jax.nn.sigmoid(gate_logits.astype(jnp.float32)).astype(q.dtype)
    o = (g[..., 0:1] * o_cmp.reshape(B, T, H, d)
         + g[..., 1:2] * o_slc.reshape(B, T, H, d)
         + g[..., 2:3] * o_win)
    return o.reshape(B, T, H * d)


def moe(h, router_w, router_bias, w_gate, w_up, w_down):
    B, T, D = h.shape
    hf = h.reshape(B * T, D)
    scores = jax.nn.sigmoid((hf @ router_w).astype(jnp.float32))
    biased = scores + router_bias.astype(jnp.float32)
    grouped = biased.reshape(-1, N_GROUPS, EXPERTS_PER_GROUP)
    group_score = jnp.sum(lax.top_k(grouped, TOP_K)[0], -1)
    grp = jnp.argmax(group_score, -1)
    in_grp = jnp.arange(N_GROUPS)[None, :] == grp[:, None]
    masked = jnp.where(in_grp[:, :, None], grouped, -jnp.inf).reshape(-1, N_EXPERTS)
    _, eidx = lax.top_k(masked, TOP_K)
    w = jnp.take_along_axis(scores, eidx, axis=1)
    w = w / jnp.sum(w, -1, keepdims=True)
    combine = jnp.sum(jax.nn.one_hot(eidx, N_EXPERTS, dtype=jnp.float32) * w[..., None], axis=1)
    g = jnp.einsum('nd,edf->nef', hf, w_gate)
    u = jnp.einsum('nd,edf->nef', hf, w_up)
    a = jax.nn.silu(g) * u * combine[:, :, None].astype(h.dtype)
    y = jnp.einsum('nef,efd->nd', a, w_down)
    return y.reshape(B, T, D)


def setup_inputs(seed: int = 0) -> dict:
    key = jax.random.key(seed)
    ks = jax.random.split(key, 32)
    f32 = jnp.float32

    def nrm(k, shape, s):
        return jax.random.normal(k, shape, f32) * s

    offs = jax.random.randint(ks[2], (BATCH, 1), 0, MAX_POS_OFFSET, dtype=jnp.int32)
    positions = jnp.arange(SEQ, dtype=jnp.int32)[None, :] + offs
    return {
        "x": nrm(ks[0], (BATCH, SEQ, D_MODEL), 1.0),
        "c": nrm(ks[1], (BATCH, D_MODEL), 1.0),
        "positions": positions,
        "ada_w": nrm(ks[3], (DEPTH, D_MODEL, 6 * D_MODEL), 0.1 * D_MODEL ** -0.5),
        "ada_b": nrm(ks[4], (DEPTH, 6 * D_MODEL), 0.02),
        "w_in": nrm(ks[5], (DEPTH, D_MODEL, D_IN), D_MODEL ** -0.5),
        "mla_q_norm": 1.0 + nrm(ks[6], (DEPTH, MLA_Q_LORA), 0.02),
        "mla_w_uq": nrm(ks[7], (DEPTH, MLA_Q_LORA, MLA_HEADS * (MLA_NOPE + MLA_ROPE)), MLA_Q_LORA ** -0.5),
        "mla_kv_norm": 1.0 + nrm(ks[8], (DEPTH, MLA_KV_LORA), 0.02),
        "mla_w_ukv": nrm(ks[9], (DEPTH, MLA_KV_LORA, MLA_HEADS * (MLA_NOPE + MLA_V)), MLA_KV_LORA ** -0.5),
        "swa_sinks": nrm(ks[10], (DEPTH, SWA_HEADS), 1.0),
        "nsa_cmp_pos": nrm(ks[11], (DEPTH, NSA_BLOCK, HEAD_DIM), 0.1),
        "nsa_phi_k1": nrm(ks[12], (DEPTH, NSA_BLOCK * HEAD_DIM, NSA_PHI_HIDDEN), (NSA_BLOCK * HEAD_DIM) ** -0.5),
        "nsa_phi_k2": nrm(ks[13], (DEPTH, NSA_PHI_HIDDEN, HEAD_DIM), NSA_PHI_HIDDEN ** -0.5),
        "nsa_phi_v1": nrm(ks[14], (DEPTH, NSA_BLOCK * HEAD_DIM, NSA_PHI_HIDDEN), (NSA_BLOCK * HEAD_DIM) ** -0.5),
        "nsa_phi_v2": nrm(ks[15], (DEPTH, NSA_PHI_HIDDEN, HEAD_DIM), NSA_PHI_HIDDEN ** -0.5),
        "w_out": nrm(ks[16], (DEPTH, D_MIX, D_MODEL), BETA * D_MIX ** -0.5),
        "ln1_g": 1.0 + nrm(ks[17], (DEPTH, D_MODEL), 0.02),
        "ln1_b": nrm(ks[18], (DEPTH, D_MODEL), 0.02),
        "ln2_g": 1.0 + nrm(ks[19], (DEPTH, D_MODEL), 0.02),
        "ln2_b": nrm(ks[20], (DEPTH, D_MODEL), 0.02),
        "router_w": nrm(ks[21], (D_MODEL, N_EXPERTS), D_MODEL ** -0.5),
        "router_bias": nrm(ks[22], (N_EXPERTS,), 0.01),
        "moe_w_gate": nrm(ks[23], (DEPTH, N_EXPERTS, D_MODEL, D_EXPERT), D_MODEL ** -0.5),
        "moe_w_up": nrm(ks[24], (DEPTH, N_EXPERTS, D_MODEL, D_EXPERT), BETA * D_MODEL ** -0.5),
        "moe_w_down": nrm(ks[25], (DEPTH, N_EXPERTS, D_EXPERT, D_MODEL), BETA * D_EXPERT ** -0.5),
    }


def reference(x, c, positions, ada_w, ada_b, w_in, mla_q_norm, mla_w_uq, mla_kv_norm, mla_w_ukv,
              swa_sinks, nsa_cmp_pos, nsa_phi_k1, nsa_phi_k2, nsa_phi_v1, nsa_phi_v2, w_out,
              ln1_g, ln1_b, ln2_g, ln2_b, router_w, router_bias, moe_w_gate, moe_w_up, moe_w_down):
    B, T, D = x.shape
    cos_h, sin_h = rope_tables(positions, HEAD_DIM)
    cos_r, sin_r = rope_tables(positions, MLA_ROPE)
    cond = jax.nn.silu(c)
    split_at = np.cumsum(IN_SIZES)[:-1].tolist()

    def heads(z, n):
        return z.reshape(B, T, n, -1)

    for l in range(DEPTH):
        mod = cond @ ada_w[l] + ada_b[l]
        sh1, sc1, gt1, sh2, sc2, gt2 = jnp.split(mod[:, None, :], 6, axis=-1)

        h = (_ln(x) * (1.0 + sc1) + sh1).astype(x.dtype)
        (cq, ckv, kpe, sq, sk, sv, nq, nkc, nvc, nks, nvs, nkw, nvw, ngate) = jnp.split(
            h @ w_in[l], split_at, axis=-1)
        o_a = mla_attention(cq, ckv, kpe, cos_r, sin_r,
                            mla_q_norm[l], mla_w_uq[l], mla_kv_norm[l], mla_w_ukv[l])
        o_b = banded_window_attention(apply_rope(heads(sq, SWA_HEADS), cos_h, sin_h),
                                      apply_rope(heads(sk, SWA_KV_HEADS), cos_h, sin_h),
                                      heads(sv, SWA_KV_HEADS), SWA_WINDOW, swa_sinks[l])
        o_c = nsa_attention(apply_rope(heads(nq, NSA_HEADS), cos_h, sin_h),
                            apply_rope(heads(nkc, NSA_KV_HEADS), cos_h, sin_h), heads(nvc, NSA_KV_HEADS),
                            apply_rope(heads(nks, NSA_KV_HEADS), cos_h, sin_h), heads(nvs, NSA_KV_HEADS),
                            apply_rope(heads(nkw, NSA_KV_HEADS), cos_h, sin_h), heads(nvw, NSA_KV_HEADS),
                            heads(ngate, NSA_HEADS),
                            nsa_cmp_pos[l], nsa_phi_k1[l], nsa_phi_k2[l], nsa_phi_v1[l], nsa_phi_v2[l])
        y = jnp.concatenate([o_a, o_b, o_c], axis=-1) @ w_out[l]
        x = layernorm(ALPHA * x + (1.0 + gt1) * y, ln1_g[l], ln1_b[l])

        h = (_ln(x) * (1.0 + sc2) + sh2).astype(x.dtype)
        y = moe(h, router_w, router_bias, moe_w_gate[l], moe_w_up[l], moe_w_down[l])
        x = layernorm(ALPHA * x + (1.0 + gt2) * y, ln2_g[l], ln2_b[l])
    return x
```

```python
import functools
import math

import numpy as np
import jax
import jax.numpy as jnp
from jax import lax
from jax.experimental import pallas as pl
from jax.experimental.pallas import tpu as pltpu

D_MODEL = 1024
DEPTH = 2
HEAD_DIM = 64
ROPE_THETA = 10000.0
LN_EPS = 1e-5
RMS_EPS = 1e-6
MLA_HEADS = 4
MLA_Q_LORA = 256
MLA_KV_LORA = 128
MLA_NOPE = 64
MLA_ROPE = 32
MLA_V = 64
SWA_HEADS = 4
SWA_KV_HEADS = 2
SWA_WINDOW = 128
NSA_HEADS = 8
NSA_KV_HEADS = 2
NSA_BLOCK = 32
NSA_TOP_N = 8
NSA_WINDOW = 128
NSA_PHI_HIDDEN = 256
FORCE_SCORE = 1e4
N_EXPERTS = 16
N_GROUPS = 4
EXPERTS_PER_GROUP = N_EXPERTS // N_GROUPS
D_EXPERT = 256
ALPHA = (2 * DEPTH) ** 0.25

IN_SIZES = (MLA_Q_LORA, MLA_KV_LORA, MLA_ROPE,
            SWA_HEADS * HEAD_DIM, SWA_KV_HEADS * HEAD_DIM, SWA_KV_HEADS * HEAD_DIM,
            NSA_HEADS * HEAD_DIM,
            NSA_KV_HEADS * HEAD_DIM, NSA_KV_HEADS * HEAD_DIM,
            NSA_KV_HEADS * HEAD_DIM, NSA_KV_HEADS * HEAD_DIM,
            NSA_KV_HEADS * HEAD_DIM, NSA_KV_HEADS * HEAD_DIM,
            NSA_HEADS * 3)
_IN_OFF = np.concatenate([[0], np.cumsum(IN_SIZES)]).tolist()
(_CQ, _CKV, _KPE, _SQ, _SK, _SV, _NQ, _NKC, _NVC, _NKS, _NVS, _NKW, _NVW, _NGATE) = _IN_OFF[:-1]

LANES = 128
NEG = -1e30
F32 = jnp.float32
BF16 = jnp.bfloat16

Z_CQ, Z_CKV, Z_KPE = 0, 256, 384
Z_ROPE = 512
Z_PLAIN = 1792
Z_COLS = 2432

A_QM, A_KM, A_QN, A_VM, A_QS = 0, 512, 1024, 1536, 1792
A_KS, A_VS, A_NKS, A_NVS, A_NKW, A_NVW = 2048, 2176, 2304, 2432, 2560, 2688
A_COLS = 2816
F_COLS = 384

VMEM_LIMIT = 56 * 1024 * 1024


def _cparams(sem):
    return pltpu.CompilerParams(dimension_semantics=sem, vmem_limit_bytes=VMEM_LIMIT)


def _nt(a, b):
    return lax.dot_general(a, b, (((1,), (1,)), ((), ())), preferred_element_type=F32)


def _mm(a, b):
    return jnp.dot(a, b, preferred_element_type=F32)


def _w_in_columns():
    idx, keep = [], []

    def add(start, n):
        idx.extend(range(start, start + n))
        keep.extend([1.0] * n)

    add(_CQ, 256)
    add(_CKV, 128)
    for _ in range(4):
        add(_KPE, 32)
    for h in (0, 2, 1, 3):
        add(_SQ + 64 * h, 64)
    add(_SK, 128)
    for h in (0, 4, 1, 5, 2, 6, 3, 7):
        add(_NQ + 64 * h, 64)
    add(_NKC, 128)
    add(_NKS, 128)
    add(_NKW, 128)
    add(_SV, 128)
    add(_NVC, 128)
    add(_NVS, 128)
    add(_NVW, 128)
    add(_NGATE, 24)
    idx.extend([0] * 104)
    keep.extend([0.0] * 104)
    assert len(idx) == Z_COLS
    return np.asarray(idx, np.int32), np.asarray(keep, np.float32)


_W_IN_IDX, _W_IN_KEEP = _w_in_columns()


def _w_uq_columns():
    per = MLA_NOPE + MLA_ROPE
    idx = []
    for h in (0, 1):
        idx.extend(range(per * h, per * h + MLA_NOPE))
    for h in range(4):
        idx.extend(range(per * h + MLA_NOPE, per * (h + 1)))
    for h in (2, 3):
        idx.extend(range(per * h, per * h + MLA_NOPE))
    return np.asarray(idx, np.int32)


def _w_ukv_columns():
    per = MLA_NOPE + MLA_V
    idx = []
    for h in range(4):
        idx.extend(range(per * h, per * h + MLA_NOPE))
    for h in range(4):
        idx.extend(range(per * h + MLA_NOPE, per * (h + 1)))
    return np.asarray(idx, np.int32)


_W_UQ_IDX = _w_uq_columns()
_W_UKV_IDX = _w_ukv_columns()
_W_OUT_IDX = np.asarray(
    list(range(256))
    + [256 + 64 * h + d for h in (0, 2, 1, 3) for d in range(64)]
    + [512 + 64 * h + d for h in (0, 4, 1, 5, 2, 6, 3, 7) for d in range(64)], np.int32)


def _blockdiag2(w):
    z = jnp.zeros_like(w)
    top = jnp.concatenate([w, z], axis=-1)
    bot = jnp.concatenate([z, w], axis=-1)
    return jnp.concatenate([top, bot], axis=-2)


def _ada_kernel(c_ref, w_ref, b_ref, o_ref):
    c = c_ref[...]
    cond = (c * jax.nn.sigmoid(c)).astype(BF16)
    o_ref[0] = _mm(cond, w_ref[0].astype(BF16)) + b_ref[0]


def _ada_mod(c, ada_w, ada_b):
    B = c.shape[0]
    tn = 512
    return pl.pallas_call(
        _ada_kernel,
        out_shape=jax.ShapeDtypeStruct((DEPTH, B, 6 * D_MODEL), F32),
        grid=(DEPTH, 6 * D_MODEL // tn),
        in_specs=[pl.BlockSpec((B, D_MODEL), lambda l, j: (0, 0)),
                  pl.BlockSpec((1, D_MODEL, tn), lambda l, j: (l, 0, j)),
                  pl.BlockSpec((1, 1, tn), lambda l, j: (l, 0, j))],
        out_specs=pl.BlockSpec((1, B, tn), lambda l, j: (l, 0, j)),
        compiler_params=_cparams(("arbitrary", "arbitrary")),
        name="ada_mod",
    )(c, ada_w, ada_b.reshape(DEPTH, 1, 6 * D_MODEL))


def _rope_kernel(pos_ref, inv_ref, o_ref):
    ang = pos_ref[...].astype(F32) * inv_ref[...]
    c = jnp.cos(ang)
    s = jnp.sin(ang)
    lane = lax.broadcasted_iota(jnp.int32, c.shape, 1)
    sign64 = jnp.where((lane & 63) < 32, -1.0, 1.0)
    sign32 = jnp.where((lane & 31) < 16, -1.0, 1.0)
    o_ref[:, 0:128] = jnp.concatenate([c[:, :64], c[:, :64]], axis=1)
    o_ref[:, 128:256] = jnp.concatenate([s[:, :64], s[:, :64]], axis=1) * sign64
    o_ref[:, 256:384] = jnp.concatenate([c[:, 64:], c[:, 64:]], axis=1)
    o_ref[:, 384:512] = jnp.concatenate([s[:, 64:], s[:, 64:]], axis=1) * sign32


def _rope_tables(positions):
    n = positions.size
    inv64 = 1.0 / (ROPE_THETA ** (jnp.arange(0, HEAD_DIM, 2, dtype=F32) / HEAD_DIM))
    inv32 = 1.0 / (ROPE_THETA ** (jnp.arange(0, MLA_ROPE, 2, dtype=F32) / MLA_ROPE))
    inv = jnp.concatenate([inv64, inv64, inv32, inv32, inv32, inv32]).reshape(1, LANES)
    tm = 1024
    return pl.pallas_call(
        _rope_kernel,
        out_shape=jax.ShapeDtypeStruct((n, 512), F32),
        grid=(n // tm,),
        in_specs=[pl.BlockSpec((tm, 1), lambda i: (i, 0)),
                  pl.BlockSpec((1, LANES), lambda i: (0, 0))],
        out_specs=pl.BlockSpec((tm, 512), lambda i: (i, 0)),
        compiler_params=_cparams(("arbitrary",)),
        name="rope_tables",
    )(positions.reshape(n, 1), inv)


def _ln(x):
    mu = jnp.mean(x, axis=-1, keepdims=True)
    xc = x - mu
    var = jnp.mean(xc * xc, axis=-1, keepdims=True)
    return xc * lax.rsqrt(var + LN_EPS)


def _rot_half(x, half):
    w = x.shape[-1]
    lane = lax.broadcasted_iota(jnp.int32, x.shape, 1)
    first = (lane & (2 * half - 1)) < half
    return jnp.where(first, pltpu.roll(x, w - half, 1), pltpu.roll(x, half, 1))


def _inproj_kernel(x_ref, mod_ref, tab_ref, w_ref, qn_ref, wuq_ref, kvn_ref, wukv_ref, a_ref, f_ref):
    sh = mod_ref[0, :, 0:D_MODEL]
    sc = mod_ref[0, :, D_MODEL:2 * D_MODEL]
    h = (_ln(x_ref[...]) * (1.0 + sc) + sh).astype(BF16)
    cos64 = tab_ref[:, 0:128]
    sin64 = tab_ref[:, 128:256]
    cos32 = tab_ref[:, 256:384]
    sin32 = tab_ref[:, 384:512]

    def rms(z, g):
        return (z * lax.rsqrt(jnp.mean(z * z, axis=-1, keepdims=True) + RMS_EPS) * g).astype(BF16)

    def rope32(z):
        return z * cos32 + _rot_half(z, 16) * sin32

    def rope64(z):
        return z * cos64 + _rot_half(z, 32) * sin64

    q = _mm(rms(_mm(h, w_ref[:, Z_CQ:Z_CQ + 256]), qn_ref[...]), wuq_ref[...])
    qpe = rope32(q[:, 128:256]).astype(BF16)
    a_ref[:, A_QM:A_QM + 128] = q[:, 0:128].astype(BF16)
    a_ref[:, A_QM + 128:A_QM + 256] = qpe
    a_ref[:, A_QM + 256:A_QM + 384] = q[:, 256:384].astype(BF16)
    a_ref[:, A_QM + 384:A_QM + 512] = qpe
    kv = _mm(rms(_mm(h, w_ref[:, Z_CKV:Z_CKV + 128]), kvn_ref[...]), wukv_ref[...])
    kpe = rope32(_mm(h, w_ref[:, Z_KPE:Z_KPE + 128])).astype(BF16)
    a_ref[:, A_KM:A_KM + 128] = kv[:, 0:128].astype(BF16)
    a_ref[:, A_KM + 128:A_KM + 256] = kpe
    a_ref[:, A_KM + 256:A_KM + 384] = kv[:, 128:256].astype(BF16)
    a_ref[:, A_KM + 384:A_KM + 512] = kpe
    a_ref[:, A_VM:A_VM + 256] = kv[:, 256:512].astype(BF16)

    zr = _mm(h, w_ref[:, Z_ROPE:Z_PLAIN])
    dst = (A_QS, A_QS + 128, A_KS, A_QN, A_QN + 128, A_QN + 256, A_QN + 384, None, A_NKS, A_NKW)
    for c, d in enumerate(dst):
        r = rope64(zr[:, 128 * c:128 * (c + 1)])
        if d is None:
            f_ref[:, 0:128] = r
        else:
            a_ref[:, d:d + 128] = r.astype(BF16)

    zp = _mm(h, w_ref[:, Z_PLAIN:Z_COLS])
    a_ref[:, A_VS:A_VS + 128] = zp[:, 0:128].astype(BF16)
    f_ref[:, 128:256] = zp[:, 128:256]
    a_ref[:, A_NVS:A_NVS + 128] = zp[:, 256:384].astype(BF16)
    a_ref[:, A_NVW:A_NVW + 128] = zp[:, 384:512].astype(BF16)
    f_ref[:, 256:384] = jax.nn.sigmoid(zp[:, 512:640])


def _inproj(x2d, mod, tab, w_in_p, q_norm, w_uq_p, kv_norm, w_ukv_p, T):
    n = x2d.shape[0]
    tm = 512
    per_b = T // tm
    const = lambda i: (0, 0)
    return pl.pallas_call(
        _inproj_kernel,
        out_shape=(jax.ShapeDtypeStruct((n, A_COLS), BF16), jax.ShapeDtypeStruct((n, F_COLS), F32)),
        grid=(n // tm,),
        in_specs=[pl.BlockSpec((tm, D_MODEL), lambda i: (i, 0)),
                  pl.BlockSpec((1, 1, 6 * D_MODEL), lambda i: (i // per_b, 0, 0)),
                  pl.BlockSpec((tm, 512), lambda i: (i, 0)),
                  pl.BlockSpec((D_MODEL, Z_COLS), const),
                  pl.BlockSpec((1, MLA_Q_LORA), const),
                  pl.BlockSpec((MLA_Q_LORA, 384), const),
                  pl.BlockSpec((1, MLA_KV_LORA), const),
                  pl.BlockSpec((MLA_KV_LORA, 512), const)],
        out_specs=(pl.BlockSpec((tm, A_COLS), lambda i: (i, 0)),
                   pl.BlockSpec((tm, F_COLS), lambda i: (i, 0))),
        compiler_params=_cparams(("arbitrary",)),
        name="in_proj",
    )(x2d, mod, tab, w_in_p, q_norm, w_uq_p, kv_norm, w_ukv_p)


def _compress_kernel(z_ref, pos_ref, w1k_ref, w1v_ref, w2k_ref, w2v_ref, ok_ref, ov_ref):
    rows = z_ref.shape[0]
    acc_k = jnp.zeros((rows, 2 * NSA_PHI_HIDDEN), F32)
    acc_v = jnp.zeros((rows, 2 * NSA_PHI_HIDDEN), F32)
    for i in range(NSA_BLOCK):
        zi = z_ref[:, i, :] + pos_ref[i]
        acc_k = acc_k + _mm(zi[:, 0:128].astype(BF16), w1k_ref[i])
        acc_v = acc_v + _mm(zi[:, 128:256].astype(BF16), w1v_ref[i])
    ok_ref[...] = _mm(jax.nn.gelu(acc_k).astype(BF16), w2k_ref[...]).astype(BF16)
    ov_ref[...] = _mm(jax.nn.gelu(acc_v).astype(BF16), w2v_ref[...]).astype(BF16)


def _compress(fslab, cmp_pos, w1k, w1v, w2k, w2v):
    n = fslab.shape[0]
    nrow = n // NSA_BLOCK
    z3 = fslab.reshape(nrow, NSA_BLOCK, F_COLS)
    rows = min(256, nrow)
    pos4 = jnp.concatenate([cmp_pos] * 4, axis=-1).reshape(NSA_BLOCK, 1, 256)
    const3 = lambda i: (0, 0, 0)
    const2 = lambda i: (0, 0)
    return pl.pallas_call(
        _compress_kernel,
        out_shape=(jax.ShapeDtypeStruct((nrow, LANES), BF16), jax.ShapeDtypeStruct((nrow, LANES), BF16)),
        grid=(nrow // rows,),
        in_specs=[pl.BlockSpec((rows, NSA_BLOCK, 256), lambda i: (i, 0, 0)),
                  pl.BlockSpec((NSA_BLOCK, 1, 256), const3),
                  pl.BlockSpec((NSA_BLOCK, LANES, 2 * NSA_PHI_HIDDEN), const3),
                  pl.BlockSpec((NSA_BLOCK, LANES, 2 * NSA_PHI_HIDDEN), const3),
                  pl.BlockSpec((2 * NSA_PHI_HIDDEN, LANES), const2),
                  pl.BlockSpec((2 * NSA_PHI_HIDDEN, LANES), const2)],
        out_specs=(pl.BlockSpec((rows, LANES), lambda i: (i, 0)),
                   pl.BlockSpec((rows, LANES), lambda i: (i, 0))),
        compiler_params=_cparams(("arbitrary",)),
        name="nsa_compress",
    )(z3, pos4, w1k, w1v, w2k, w2v)


def _lane_lo(shape):
    return lax.broadcasted_iota(jnp.int32, shape, 1) < 64


def _mla_kernel(q_ref, k_ref, v_ref, o_ref):
    tq = q_ref.shape[1]
    tk = tq
    qi = pl.program_id(1)
    scale = 1.0 / math.sqrt(MLA_NOPE + MLA_ROPE)
    lane = lax.broadcasted_iota(jnp.int32, (tq, 256), 1)
    row = lax.broadcasted_iota(jnp.int32, (2 * tq, tk), 0) & (tq - 1)
    col = lax.broadcasted_iota(jnp.int32, (2 * tq, tk), 1)
    causal = col <= row
    lo = _lane_lo((tq, LANES))
    for p in range(2):
        qp = q_ref[0, :, 256 * p:256 * (p + 1)]
        heads = []
        for e in range(2):
            h = 2 * p + e
            keep = ((lane >= 64 * e) & (lane < 64 * e + 64)) | ((lane >= 128 + 32 * h) & (lane < 160 + 32 * h))
            heads.append(jnp.where(keep, qp, jnp.zeros_like(qp)))
        q2 = jnp.concatenate(heads, axis=0)

        def step(j, carry, masked):
            m, l, acc = carry
            start = pl.multiple_of(j * tk, tk)
            kj = k_ref[0, pl.ds(start, tk), 256 * p:256 * (p + 1)]
            vj = v_ref[0, pl.ds(start, tk), 128 * p:128 * (p + 1)]
            s = _nt(q2, kj) * scale
            if masked:
                s = jnp.where(causal, s, NEG)
            m_new = jnp.maximum(m, jnp.max(s, axis=1, keepdims=True))
            a = jnp.exp(m - m_new)
            e_ = jnp.exp(s - m_new)
            l = a * l + jnp.sum(e_, axis=1, keepdims=True)
            acc = a * acc + _mm(e_.astype(BF16), vj)
            return m_new, l, acc

        init = (jnp.full((2 * tq, 1), NEG, F32), jnp.zeros((2 * tq, 1), F32), jnp.zeros((2 * tq, LANES), F32))
        carry = lax.fori_loop(0, qi, functools.partial(step, masked=False), init)
        m, l, acc = step(qi, carry, True)
        o = acc / l
        o_ref[0, :, 128 * p:128 * (p + 1)] = jnp.where(lo, o[0:tq], o[tq:2 * tq]).astype(BF16)


def _mla_attention(acts, B, T):
    tq = 256
    return pl.pallas_call(
        _mla_kernel,
        out_shape=jax.ShapeDtypeStruct((B, T, 256), BF16),
        grid=(B, T // tq),
        in_specs=[pl.BlockSpec((1, tq, 512), lambda b, i: (b, i, A_QM // 512)),
                  pl.BlockSpec((1, T, 512), lambda b, i: (b, 0, A_KM // 512)),
                  pl.BlockSpec((1, T, 256), lambda b, i: (b, 0, A_VM // 256))],
        out_specs=pl.BlockSpec((1, tq, 256), lambda b, i: (b, i, 0)),
        compiler_params=_cparams(("arbitrary", "arbitrary")),
        name="mla_attention",
    )(acts, acts, acts)


def _stack_heads(q_ref, ncol):
    tq = q_ref.shape[1]
    lo = _lane_lo((tq, LANES))
    cols = [q_ref[0, :, 128 * c:128 * (c + 1)] for c in range(ncol)]
    zero = jnp.zeros((tq, LANES), BF16)
    return jnp.concatenate([jnp.where(lo, c, zero) for c in cols] + [jnp.where(lo, zero, c) for c in cols], axis=0)


def _window_scores(qs, k_ref, i, tq):
    ws = pl.multiple_of(jnp.maximum(i - 1, 0) * tq, tq)
    kw = k_ref[0, pl.ds(ws, 2 * tq), :]
    s = _nt(qs, kw) * (1.0 / math.sqrt(HEAD_DIM))
    row = lax.broadcasted_iota(jnp.int32, s.shape, 0) & (tq - 1)
    col = lax.broadcasted_iota(jnp.int32, s.shape, 1)
    rel = (i * tq + row) - (ws + col)
    return jnp.where((rel >= 0) & (rel < tq), s, NEG), ws


def _swa_kernel(sink_ref, q_ref, k_ref, v_ref, o_ref):
    tq = q_ref.shape[1]
    i = pl.program_id(1)
    qs = _stack_heads(q_ref, 2)
    s, ws = _window_scores(qs, k_ref, i, tq)
    sink = jnp.concatenate([jnp.full((tq, 1), sink_ref[h], F32) for h in (0, 1, 2, 3)], axis=0)
    m = jnp.maximum(jnp.max(s, axis=1, keepdims=True), sink)
    e = jnp.exp(s - m)
    den = jnp.sum(e, axis=1, keepdims=True) + jnp.exp(sink - m)
    o = _mm(e.astype(BF16), v_ref[0, pl.ds(ws, 2 * tq), :]) / den
    lo = _lane_lo((tq, LANES))
    for c in range(2):
        o_ref[0, :, 128 * c:128 * (c + 1)] = jnp.where(lo, o[c * tq:(c + 1) * tq],
                                                     o[(2 + c) * tq:(3 + c) * tq]).astype(BF16)


def _swa_attention(acts, sinks, B, T):
    tq = SWA_WINDOW
    return pl.pallas_call(
        _swa_kernel,
        out_shape=jax.ShapeDtypeStruct((B, T, 256), BF16),
        grid=(B, T // tq),
        in_specs=[pl.BlockSpec(memory_space=pltpu.SMEM),
                  pl.BlockSpec((1, tq, 256), lambda b, i: (b, i, A_QS // 256)),
                  pl.BlockSpec((1, T, LANES), lambda b, i: (b, 0, A_KS // LANES)),
                  pl.BlockSpec((1, T, LANES), lambda b, i: (b, 0, A_VS // LANES))],
        out_specs=pl.BlockSpec((1, tq, 256), lambda b, i: (b, i, 0)),
        compiler_params=_cparams(("arbitrary", "arbitrary")),
        name="swa_attention",
    )(sinks, acts, acts, acts)


def _nsa_kernel(q_ref, kc_ref, vc_ref, ks_ref, vs_ref, kw_ref, vw_ref, g_ref, o_ref):
    tq = q_ref.shape[1]
    tk = 2 * tq
    nc = kc_ref.shape[1]
    i = pl.program_id(1)
    q0 = i * tq
    scale = 1.0 / math.sqrt(HEAD_DIM)
    qs = _stack_heads(q_ref, 4)
    rows = 8 * tq
    lo = _lane_lo((tq, LANES))

    s = _nt(qs, kc_ref[0]) * scale
    tpos = q0 + (lax.broadcasted_iota(jnp.int32, (rows, nc), 0) & (tq - 1))
    blk = lax.broadcasted_iota(jnp.int32, (rows, nc), 1)
    s = jnp.where(blk * NSA_BLOCK + (NSA_BLOCK - 1) <= tpos, s, NEG)
    m = jnp.max(s, axis=1, keepdims=True)
    m = jnp.where(m < 0.5 * NEG, 0.0, m)
    e = jnp.exp(s - m)
    den = jnp.sum(e, axis=1, keepdims=True)
    p = e / jnp.where(den > 0.0, den, 1.0)
    o_cmp = _mm(p.astype(BF16), vc_ref[0])

    imp = jnp.concatenate([p[0:tq] + p[tq:2 * tq] + p[2 * tq:3 * tq] + p[3 * tq:4 * tq],
                           p[4 * tq:5 * tq] + p[5 * tq:6 * tq] + p[6 * tq:7 * tq] + p[7 * tq:8 * tq]], axis=1)
    t = q0 + lax.broadcasted_iota(jnp.int32, (tq, LANES), 0)
    j = lax.broadcasted_iota(jnp.int32, (tq, LANES), 1) & (nc - 1)
    tb = t >> 5
    future = j * NSA_BLOCK > t
    forced = (j == 0) | (j == tb) | (j == tb - 1)
    val = jnp.where(future, -1.0, jnp.where(forced, FORCE_SCORE, imp))
    rank = jnp.zeros((tq, LANES), F32)
    for jp in range(nc):
        other = jnp.where(lo, val[:, jp:jp + 1], val[:, nc + jp:nc + jp + 1])
        ahead = jnp.where(other > val, 1.0, jnp.where(other == val, jnp.where(j > jp, 1.0, 0.0), 0.0))
        rank = rank + ahead
    member = jnp.where(future, 0.0, jnp.where(rank < NSA_TOP_N, 1.0, 0.0)).astype(BF16)

    erow = lax.broadcasted_iota(jnp.int32, (LANES, tk), 0)
    ecol = lax.broadcasted_iota(jnp.int32, (LANES, tk), 1) >> 5
    qrow = lax.broadcasted_iota(jnp.int32, (tq, tk), 0)
    kcol = lax.broadcasted_iota(jnp.int32, (tq, tk), 1)

    def slc_step(jt, carry, diag):
        m_, l_, acc = carry
        start = pl.multiple_of(jt * tk, tk)
        s_ = _nt(qs, ks_ref[0, pl.ds(start, tk), :]) * scale
        first = jt * (tk // NSA_BLOCK)
        allow = []
        for g in range(2):
            expand = jnp.where(erow == nc * g + first + ecol, 1.0, 0.0).astype(BF16)
            a = _mm(member, expand)
            if diag:
                a = jnp.where(start + kcol <= q0 + qrow, a, 0.0)
            allow.append(a)
        allow = jnp.concatenate([allow[0]] * 4 + [allow[1]] * 4, axis=0)
        s_ = jnp.where(allow > 0.5, s_, NEG)
        m_new = jnp.maximum(m_, jnp.max(s_, axis=1, keepdims=True))
        al = jnp.exp(m_ - m_new)
        e_ = jnp.exp(s_ - m_new)
        l_ = al * l_ + jnp.sum(e_, axis=1, keepdims=True)
        acc = al * acc + _mm(e_.astype(BF16), vs_ref[0, pl.ds(start, tk), :])
        return m_new, l_, acc

    init = (jnp.full((rows, 1), NEG, F32), jnp.zeros((rows, 1), F32), jnp.zeros((rows, LANES), F32))
    last = i >> 1
    carry = lax.fori_loop(0, last, functools.partial(slc_step, diag=False), init)
    m_, l_, acc = slc_step(last, carry, True)
    o_slc = acc / l_

    s, ws = _window_scores(qs, kw_ref, i, tq)
    m = jnp.max(s, axis=1, keepdims=True)
    e = jnp.exp(s - m)
    o_win = _mm(e.astype(BF16), vw_ref[0, pl.ds(ws, tk), :]) / jnp.sum(e, axis=1, keepdims=True)

    gate = g_ref[0]
    for c in range(4):
        out = jnp.zeros((tq, LANES), F32)
        for br, o in enumerate((o_cmp, o_slc, o_win)):
            g = jnp.where(lo, gate[:, 3 * c + br:3 * c + br + 1], gate[:, 3 * (c + 4) + br:3 * (c + 4) + br + 1])
            out = out + g * jnp.where(lo, o[c * tq:(c + 1) * tq], o[(c + 4) * tq:(c + 5) * tq])
        o_ref[0, :, 128 * c:128 * (c + 1)] = out.astype(BF16)


def _nsa_attention(acts, fslab, k_cmp, v_cmp, B, T):
    tq = NSA_WINDOW
    nc = T // NSA_BLOCK
    full = lambda off: pl.BlockSpec((1, T, LANES), lambda b, i: (b, 0, off // LANES))
    return pl.pallas_call(
        _nsa_kernel,
        out_shape=jax.ShapeDtypeStruct((B, T, 512), BF16),
        grid=(B, T // tq),
        in_specs=[pl.BlockSpec((1, tq, 512), lambda b, i: (b, i, A_QN // 512)),
                  pl.BlockSpec((1, nc, LANES), lambda b, i: (b, 0, 0)),
                  pl.BlockSpec((1, nc, LANES), lambda b, i: (b, 0, 0)),
                  full(A_NKS), full(A_NVS), full(A_NKW), full(A_NVW),
                  pl.BlockSpec((1, tq, LANES), lambda b, i: (b, i, 2))],
        out_specs=pl.BlockSpec((1, tq, 512), lambda b, i: (b, i, 0)),
        compiler_params=_cparams(("arbitrary", "arbitrary")),
        name="nsa_attention",
    )(acts, k_cmp, v_cmp, acts, acts, acts, acts, fslab)


def _outproj_kernel(x_ref, mod_ref, oa_ref, ob_ref, oc_ref, w_ref, g_ref, b_ref, o_ref):
    y = (_mm(oa_ref[...], w_ref[0:256, :]) + _mm(ob_ref[...], w_ref[256:512, :])
         + _mm(oc_ref[...], w_ref[512:1024, :]))
    gt = mod_ref[0, :, 2 * D_MODEL:3 * D_MODEL]
    o_ref[...] = _ln(ALPHA * x_ref[...] + (1.0 + gt) * y) * g_ref[...] + b_ref[...]


def _outproj(x2d, mod, o_a, o_b, o_c, w_out_p, g, b, T):
    n = x2d.shape[0]
    tm = 512
    per_b = T // tm
    const = lambda i: (0, 0)
    row = lambda w: pl.BlockSpec((tm, w), lambda i: (i, 0))
    return pl.pallas_call(
        _outproj_kernel,
        out_shape=jax.ShapeDtypeStruct((n, D_MODEL), F32),
        grid=(n // tm,),
        in_specs=[row(D_MODEL),
                  pl.BlockSpec((1, 1, 6 * D_MODEL), lambda i: (i // per_b, 0, 0)),
                  row(256), row(256), row(512),
                  pl.BlockSpec((D_MODEL, D_MODEL), const),
                  pl.BlockSpec((1, D_MODEL), const),
                  pl.BlockSpec((1, D_MODEL), const)],
        out_specs=row(D_MODEL),
        compiler_params=_cparams(("arbitrary",)),
        name="out_proj",
    )(x2d, mod, o_a, o_b, o_c, w_out_p, g, b)


def _split_bf16(x):
    hi = x.astype(BF16)
    return hi, (x - hi.astype(F32)).astype(BF16)


def _route(h, rw_ref, rb_ref):
    h_hi, h_lo = _split_bf16(h)
    w_hi, w_lo = _split_bf16(rw_ref[...])
    logits = _mm(h_hi, w_hi) + (_mm(h_hi, w_lo) + _mm(h_lo, w_hi))
    scores = jax.nn.sigmoid(logits)
    biased = scores + rb_ref[...]
    lane = lax.broadcasted_iota(jnp.int32, biased.shape, 1).astype(F32)
    ninf = -jnp.inf

    def top2(v):
        m1 = jnp.max(v, axis=1, keepdims=True)
        i1 = jnp.min(jnp.where(v == m1, lane, float(N_EXPERTS)), axis=1, keepdims=True)
        v2 = jnp.where(lane == i1, ninf, v)
        m2 = jnp.max(v2, axis=1, keepdims=True)
        i2 = jnp.min(jnp.where(v2 == m2, lane, float(N_EXPERTS)), axis=1, keepdims=True)
        return m1, i1, m2, i2

    grp_of_lane = jnp.floor(lane * (1.0 / EXPERTS_PER_GROUP))
    best = None
    grp = None
    for g in range(N_GROUPS):
        m1, _, m2, _ = top2(jnp.where(grp_of_lane == g, biased, ninf))
        gs = m1 + m2
        if g == 0:
            best, grp = gs, jnp.zeros_like(gs)
        else:
            better = gs > best
            grp = jnp.where(better, float(g), grp)
            best = jnp.where(better, gs, best)
    _, i1, _, i2 = top2(jnp.where(grp_of_lane == grp, biased, ninf))
    w1 = jnp.sum(jnp.where(lane == i1, scores, 0.0), axis=1, keepdims=True)
    w2 = jnp.sum(jnp.where(lane == i2, scores, 0.0), axis=1, keepdims=True)
    tot = w1 + w2
    return jnp.where(lane == i1, w1 / tot, 0.0) + jnp.where(lane == i2, w2 / tot, 0.0)


def _moe_kernel(x_ref, mod_ref, rw_ref, rb_ref, wg_ref, wu_ref, wd_ref, g_ref, b_ref, o_ref, h_s, c_s):
    e = pl.program_id(1)

    @pl.when(e == 0)
    def _():
        sh = mod_ref[0, :, 3 * D_MODEL:4 * D_MODEL]
        sc = mod_ref[0, :, 4 * D_MODEL:5 * D_MODEL]
        h = _ln(x_ref[...]) * (1.0 + sc) + sh
        h_s[...] = h.astype(BF16)
        c_s[...] = _route(h, rw_ref, rb_ref)
        o_ref[...] = jnp.zeros_like(o_ref)

    h = h_s[...]
    comb = c_s[...]
    lane = lax.broadcasted_iota(jnp.int32, comb.shape, 1)
    ce = jnp.sum(jnp.where(lane == e, comb, 0.0), axis=1, keepdims=True)
    gate = _mm(h, wg_ref[0])
    up = _mm(h, wu_ref[0])
    a = (gate * jax.nn.sigmoid(gate)) * up * ce
    o_ref[...] += _mm(a.astype(BF16), wd_ref[0])

    @pl.when(e == N_EXPERTS - 1)
    def _():
        gt = mod_ref[0, :, 5 * D_MODEL:6 * D_MODEL]
        o_ref[...] = _ln(ALPHA * x_ref[...] + (1.0 + gt) * o_ref[...]) * g_ref[...] + b_ref[...]


def _moe(x2d, mod, router_w, router_bias, wg, wu, wd, g, b, T):
    n = x2d.shape[0]
    tm = 512
    per_b = T // tm
    const = lambda i, e: (0, 0)
    return pl.pallas_call(
        _moe_kernel,
        out_shape=jax.ShapeDtypeStruct((n, D_MODEL), F32),
        grid=(n // tm, N_EXPERTS),
        in_specs=[pl.BlockSpec((tm, D_MODEL), lambda i, e: (i, 0)),
                  pl.BlockSpec((1, 1, 6 * D_MODEL), lambda i, e: (i // per_b, 0, 0)),
                  pl.BlockSpec((D_MODEL, N_EXPERTS), const),
                  pl.BlockSpec((1, N_EXPERTS), const),
                  pl.BlockSpec((1, D_MODEL, D_EXPERT), lambda i, e: (e, 0, 0)),
                  pl.BlockSpec((1, D_MODEL, D_EXPERT), lambda i, e: (e, 0, 0)),
                  pl.BlockSpec((1, D_EXPERT, D_MODEL), lambda i, e: (e, 0, 0)),
                  pl.BlockSpec((1, D_MODEL), const),
                  pl.BlockSpec((1, D_MODEL), const)],
        out_specs=pl.BlockSpec((tm, D_MODEL), lambda i, e: (i, 0)),
        scratch_shapes=[pltpu.VMEM((tm, D_MODEL), BF16), pltpu.VMEM((tm, N_EXPERTS), F32)],
        compiler_params=_cparams(("arbitrary", "arbitrary")),
        name="moe",
    )(x2d, mod, router_w, router_bias, wg, wu, wd, g, b)


def kernel(x, c, positions, ada_w, ada_b, w_in, mla_q_norm, mla_w_uq, mla_kv_norm, mla_w_ukv,
           swa_sinks, nsa_cmp_pos, nsa_phi_k1, nsa_phi_k2, nsa_phi_v1, nsa_phi_v2, w_out,
           ln1_g, ln1_b, ln2_g, ln2_b, router_w, router_bias, moe_w_gate, moe_w_up, moe_w_down):
    B, T, D = x.shape
    assert D == D_MODEL and T // NSA_BLOCK == 64 and T % 512 == 0
    n = B * T
    mod_all = _ada_mod(c, ada_w, ada_b)
    tab = _rope_tables(positions)
    rb = router_bias.reshape(1, N_EXPERTS).astype(F32)
    xf = x.reshape(n, D)
    for l in range(DEPTH):
        mod = mod_all[l].reshape(B, 1, 6 * D)
        w_in_p = (jnp.take(w_in[l], _W_IN_IDX, axis=1) * _W_IN_KEEP).astype(BF16)
        w_uq_p = jnp.take(mla_w_uq[l], _W_UQ_IDX, axis=1).astype(BF16)
        w_ukv_p = jnp.take(mla_w_ukv[l], _W_UKV_IDX, axis=1).astype(BF16)
        w_out_p = jnp.take(w_out[l], _W_OUT_IDX, axis=0).astype(BF16)
        w1k = _blockdiag2(nsa_phi_k1[l].reshape(NSA_BLOCK, HEAD_DIM, NSA_PHI_HIDDEN)).astype(BF16)
        w1v = _blockdiag2(nsa_phi_v1[l].reshape(NSA_BLOCK, HEAD_DIM, NSA_PHI_HIDDEN)).astype(BF16)
        w2k = _blockdiag2(nsa_phi_k2[l]).astype(BF16)
        w2v = _blockdiag2(nsa_phi_v2[l]).astype(BF16)

        acts, fslab = _inproj(xf, mod, tab, w_in_p, mla_q_norm[l].reshape(1, -1), w_uq_p,
                              mla_kv_norm[l].reshape(1, -1), w_ukv_p, T)
        k_cmp, v_cmp = _compress(fslab, nsa_cmp_pos[l], w1k, w1v, w2k, w2v)
        acts3 = acts.reshape(B, T, A_COLS)
        o_a = _mla_attention(acts3, B, T)
        o_b = _swa_attention(acts3, swa_sinks[l], B, T)
        o_c = _nsa_attention(acts3, fslab.reshape(B, T, F_COLS), k_cmp.reshape(B, T // NSA_BLOCK, LANES),
                             v_cmp.reshape(B, T // NSA_BLOCK, LANES), B, T)
        xf = _outproj(xf, mod, o_a.reshape(n, 256), o_b.reshape(n, 256), o_c.reshape(n, 512), w_out_p,
                      ln1_g[l].reshape(1, D), ln1_b[l].reshape(1, D), T)
        xf = _moe(xf, mod, router_w, rb, moe_w_gate[l].astype(BF16), moe_w_up[l].astype(BF16),
                  moe_w_down[l].astype(BF16), ln2_g[l].reshape(1, D), ln2_b[l].reshape(1, D), T)
    return xf.reshape(B, T, D)
```
